```python
import math
import jax, jax.numpy as jnp
from jax import lax
import numpy as np


D_MODEL = 1024
BATCH = 8
SEQ = 2048
DEPTH = 2

POOL_WINDOWS = (2, 4, 8, 16)
POOL_GROUPS = len(POOL_WINDOWS)
POOL_WIDTH = D_MODEL // 2
POOL_GROUP_DIM = POOL_WIDTH // POOL_GROUPS
DIFF_HEADS = 4
DIFF_WIDTH = D_MODEL - POOL_WIDTH
DIFF_VDIM = DIFF_WIDTH // DIFF_HEADS
DIFF_QKDIM = DIFF_VDIM // 2
AB_IN = POOL_WIDTH + 3 * DIFF_WIDTH
Q_BLOCK = 128
ROPE_THETA = 10000.0

MLSTM_HEADS = 4
MLSTM_V_WIDTH = D_MODEL
MLSTM_VDIM = MLSTM_V_WIDTH // MLSTM_HEADS
MLSTM_QKDIM = MLSTM_VDIM // 2
MLSTM_QK_WIDTH = MLSTM_HEADS * MLSTM_QKDIM
MLSTM_IN = 2 * MLSTM_QK_WIDTH + 2 * MLSTM_V_WIDTH + 2 * MLSTM_HEADS
MLSTM_CONV = 4
MLSTM_CHUNK = 64

D_FF = 2816
FFN_CONV = 3

N_EVEN = (DEPTH + 1) // 2
N_ODD = DEPTH // 2
RMS_EPS = 1e-6
NEG_INF = -1e30

kernel_name = 'hybrid_pool_diffattn_mlstm_convffn'


def rms_norm(x, g):
    xf = x.astype(jnp.float32)
    y = xf * lax.rsqrt(jnp.mean(xf * xf, axis=-1, keepdims=True) + RMS_EPS)
    return (y * g.astype(jnp.float32)).astype(x.dtype)


def causal_depthwise_conv(x, w, b):
    width, ch = w.shape
    y = lax.conv_general_dilated(
        x, w[:, None, :].astype(x.dtype), window_strides=(1,),
        padding=[(width - 1, 0)], dimension_numbers=('NWC', 'WIO', 'NWC'),
        feature_group_count=ch)
    return y + b.astype(x.dtype)


def rope_tables(positions, dim):
    inv_freq = ROPE_THETA ** (-jnp.arange(0, dim, 2, dtype=jnp.float32) / dim)
    ang = positions.astype(jnp.float32)[..., None] * inv_freq
    return jnp.cos(ang), jnp.sin(ang)


def apply_rope(x, cos, sin):
    x = x.astype(jnp.float32)
    x1, x2 = jnp.split(x, 2, axis=-1)
    return jnp.concatenate([x1 * cos - x2 * sin, x2 * cos + x1 * sin], axis=-1)


def pool_mixer(u, w_pool, scale):
    B, S, _ = u.shape
    uf = u.astype(jnp.float32)
    cs = jnp.cumsum(uf, axis=1)
    t = jnp.arange(S)
    pooled = []
    for g, w in enumerate(POOL_WINDOWS):
        c = cs[..., g * POOL_GROUP_DIM:(g + 1) * POOL_GROUP_DIM]
        lag = jnp.pad(c[:, :S - w], ((0, 0), (w, 0), (0, 0)))
        cnt = jnp.minimum(t + 1, w).astype(jnp.float32)[None, :, None]
        pooled.append((c - lag) / cnt)
    r = (jnp.concatenate(pooled, axis=-1) - uf).reshape(B, S, POOL_GROUPS, POOL_GROUP_DIM)
    y = jnp.einsum('bsgc,gcd->bsgd', r, w_pool.astype(jnp.float32)).reshape(B, S, POOL_WIDTH)
    return (y * scale.astype(jnp.float32)).astype(u.dtype)


def diff_attention(qkv, cos, sin, q_g, k_g, lam, lam_init, subln_g):
    B, S, _ = qkv.shape
    q, k, v = jnp.split(qkv, 3, axis=-1)
    q = q.reshape(B, S, DIFF_HEADS, 2, DIFF_QKDIM)
    k = k.reshape(B, S, DIFF_HEADS, 2, DIFF_QKDIM)
    v = v.reshape(B, S, DIFF_HEADS, DIFF_VDIM)
    q = apply_rope(rms_norm(q, q_g), cos, sin).transpose(0, 2, 3, 1, 4)
    k = apply_rope(rms_norm(k, k_g), cos, sin).transpose(0, 2, 3, 1, 4)
    vf = v.astype(jnp.float32).transpose(0, 2, 1, 3)
    nb = S // Q_BLOCK
    qb = jnp.moveaxis(q.reshape(B, DIFF_HEADS, 2, nb, Q_BLOCK, DIFF_QKDIM), 3, 0)
    scale = DIFF_QKDIM ** -0.5
    kpos = jnp.arange(S)

    def block(args):
        qi, start = args
        s = jnp.einsum('bhcqd,bhckd->bhcqk', qi, k) * scale
        qpos = start + jnp.arange(Q_BLOCK)
        s = jnp.where(kpos[None, :] <= qpos[:, None], s, NEG_INF)
        p = jax.nn.softmax(s, axis=-1)
        w = p[:, :, 0] - lam * p[:, :, 1]
        return jnp.einsum('bhqk,bhkd->bhqd', w, vf)

    o = lax.map(block, (qb, jnp.arange(nb) * Q_BLOCK))
    o = o.transpose(1, 0, 3, 2, 4).reshape(B, S, DIFF_HEADS, DIFF_VDIM)
    o = rms_norm(o, subln_g) * (1.0 - lam_init)
    return o.reshape(B, S, DIFF_WIDTH).astype(qkv.dtype)


def pool_diffattn_layer(x, cos, sin, layer_idx, norm_g, w_in, pool_w, pool_scale, q_g, k_g,
                        lq1, lk1, lq2, lk2, subln_g, w_out):
    z = rms_norm(x, norm_g) @ w_in
    lam_init = 0.8 - 0.6 * math.exp(-0.3 * layer_idx)
    lam = (jnp.exp(jnp.sum(lq1.astype(jnp.float32) * lk1.astype(jnp.float32)))
           - jnp.exp(jnp.sum(lq2.astype(jnp.float32) * lk2.astype(jnp.float32))) + lam_init)
    a = pool_mixer(z[..., :POOL_WIDTH], pool_w, pool_scale)
    b = diff_attention(z[..., POOL_WIDTH:], cos, sin, q_g, k_g, lam, lam_init, subln_g)
    return jnp.concatenate([a, b], axis=-1) @ w_out


def mlstm_chunkwise(q, k, v, li, lf):
    B, H, S, dk = q.shape
    dv = v.shape[-1]
    L = MLSTM_CHUNK
    nc = S // L

    def to_chunks(a):
        return jnp.moveaxis(a.reshape(B, H, nc, L, *a.shape[3:]), 2, 0)

    tri = jnp.arange(L)[:, None] >= jnp.arange(L)[None, :]

    def step(carry, inp):
        C, n, m = carry
        qc, kc, vc, lic, lfc = inp
        b = jnp.cumsum(lfc, axis=-1)
        a = b + m[..., None]
        dmat = jnp.where(tri, b[..., :, None] - b[..., None, :] + lic[..., None, :], NEG_INF)
        m_t = jnp.maximum(a, jnp.max(dmat, axis=-1))
        sc = jnp.einsum('bhtd,bhsd->bhts', qc, kc) * jnp.exp(dmat - m_t[..., None])
        inter = jnp.exp(a - m_t)
        num = (jnp.einsum('bhts,bhsv->bhtv', sc, vc)
               + inter[..., None] * jnp.einsum('bhtd,bhdv->bhtv', qc, C))
        den = jnp.sum(sc, axis=-1) + inter * jnp.einsum('bhtd,bhd->bht', qc, n)
        h = num / jnp.maximum(jnp.abs(den), jnp.exp(-m_t))[..., None]
        b_last = b[..., -1]
        w = b_last[..., None] - b + lic
        m_new = jnp.maximum(b_last + m, jnp.max(w, axis=-1))
        decay = jnp.exp(b_last + m - m_new)
        ws = jnp.exp(w - m_new[..., None])
        C_new = decay[..., None, None] * C + jnp.einsum('bhs,bhsd,bhsv->bhdv', ws, kc, vc)
        n_new = decay[..., None] * n + jnp.einsum('bhs,bhsd->bhd', ws, kc)
        return (C_new, n_new, m_new), h

    init = (jnp.zeros((B, H, dk, dv), jnp.float32), jnp.zeros((B, H, dk), jnp.float32),
            jnp.zeros((B, H), jnp.float32))
    _, hs = lax.scan(step, init, (to_chunks(q), to_chunks(k), to_chunks(v), to_chunks(li), to_chunks(lf)))
    return jnp.moveaxis(hs, 0, 2).reshape(B, H, S, dv)


def mlstm_layer(x, norm_g, w_in, conv_w, conv_b, gate_b, out_g, w_out):
    z = rms_norm(x, norm_g) @ w_in
    B, S, _ = z.shape
    qk = z[..., :2 * MLSTM_QK_WIDTH]
    v = z[..., 2 * MLSTM_QK_WIDTH:2 * MLSTM_QK_WIDTH + MLSTM_V_WIDTH]
    o = z[..., 2 * MLSTM_QK_WIDTH + MLSTM_V_WIDTH:2 * MLSTM_QK_WIDTH + 2 * MLSTM_V_WIDTH]
    gates = z[..., 2 * MLSTM_QK_WIDTH + 2 * MLSTM_V_WIDTH:].astype(jnp.float32) + gate_b.astype(jnp.float32)
    qk = jax.nn.silu(causal_depthwise_conv(qk, conv_w, conv_b)).astype(jnp.float32)
    q = qk[..., :MLSTM_QK_WIDTH].reshape(B, S, MLSTM_HEADS, MLSTM_QKDIM).transpose(0, 2, 1, 3)
    k = qk[..., MLSTM_QK_WIDTH:].reshape(B, S, MLSTM_HEADS, MLSTM_QKDIM).transpose(0, 2, 1, 3)
    k = k * (MLSTM_QKDIM ** -0.5)
    vh = v.astype(jnp.float32).reshape(B, S, MLSTM_HEADS, MLSTM_VDIM).transpose(0, 2, 1, 3)
    li = gates[..., :MLSTM_HEADS].transpose(0, 2, 1)
    lf = jax.nn.log_sigmoid(gates[..., MLSTM_HEADS:]).transpose(0, 2, 1)
    h = mlstm_chunkwise(q, k, vh, li, lf).transpose(0, 2, 1, 3)
    h = rms_norm(h, out_g.reshape(MLSTM_HEADS, MLSTM_VDIM)).reshape(B, S, MLSTM_V_WIDTH)
    h = h * jax.nn.sigmoid(o.astype(jnp.float32))
    return h.astype(x.dtype) @ w_out


def conv_ffn(x, norm_g, w_up, conv_w, conv_b, w_down):
    h = causal_depthwise_conv(rms_norm(x, norm_g) @ w_up, conv_w, conv_b)
    g, u = jnp.split(h, 2, axis=-1)
    return (jax.nn.silu(g) * u) @ w_down


def setup_inputs(seed: int = 0) -> dict:
    key = jax.random.key(seed)
    ks = iter(jax.random.split(key, 40))
    nrm = lambda shape, s: jax.random.normal(next(ks), shape, jnp.float32) * s
    gain = lambda shape: 1.0 + nrm(shape, 0.02)
    D = D_MODEL
    x = jax.random.normal(next(ks), (BATCH, SEQ, D), jnp.float32)
    positions = jnp.broadcast_to(jnp.arange(SEQ, dtype=jnp.int32), (BATCH, SEQ))
    f_bias = jnp.linspace(3.0, 6.0, MLSTM_HEADS, dtype=jnp.float32)[None, :] + nrm((N_ODD, MLSTM_HEADS), 0.1)
    i_bias = nrm((N_ODD, MLSTM_HEADS), 0.1)
    return {
        'x': x,
        'positions': positions,
        'ab_norm_g': gain((N_EVEN, D)),
        'ab_w_in': nrm((N_EVEN, D, AB_IN), D ** -0.5),
        'pool_w': nrm((N_EVEN, POOL_GROUPS, POOL_GROUP_DIM, POOL_GROUP_DIM), POOL_GROUP_DIM ** -0.5),
        'pool_scale': 1.0 + nrm((N_EVEN, POOL_WIDTH), 0.1),
        'q_norm_g': gain((N_EVEN, DIFF_QKDIM)),
        'k_norm_g': gain((N_EVEN, DIFF_QKDIM)),
        'lambda_q1': nrm((N_EVEN, DIFF_QKDIM), 0.1),
        'lambda_k1': nrm((N_EVEN, DIFF_QKDIM), 0.1),
        'lambda_q2': nrm((N_EVEN, DIFF_QKDIM), 0.1),
        'lambda_k2': nrm((N_EVEN, DIFF_QKDIM), 0.1),
        'subln_g': gain((N_EVEN, DIFF_VDIM)),
        'ab_w_out': nrm((N_EVEN, D, D), D ** -0.5),
        'ml_norm_g': gain((N_ODD, D)),
        'ml_w_in': nrm((N_ODD, D, MLSTM_IN), D ** -0.5),
        'ml_conv_w': nrm((N_ODD, MLSTM_CONV, 2 * MLSTM_QK_WIDTH), MLSTM_CONV ** -0.5),
        'ml_conv_b': nrm((N_ODD, 2 * MLSTM_QK_WIDTH), 0.02),
        'ml_gate_b': jnp.concatenate([i_bias, f_bias], axis=-1),
        'ml_out_norm_g': gain((N_ODD, MLSTM_V_WIDTH)),
        'ml_w_out': nrm((N_ODD, MLSTM_V_WIDTH, D), MLSTM_V_WIDTH ** -0.5),
        'ffn_norm_g': gain((DEPTH, D)),
        'ffn_w_up': nrm((DEPTH, D, 2 * D_FF), D ** -0.5),
        'ffn_conv_w': nrm((DEPTH, FFN_CONV, 2 * D_FF), FFN_CONV ** -0.5),
        'ffn_conv_b': nrm((DEPTH, 2 * D_FF), 0.02),
        'ffn_w_down': nrm((DEPTH, D_FF, D), D_FF ** -0.5),
    }


def reference(x, positions, ab_norm_g, ab_w_in, pool_w, pool_scale, q_norm_g, k_norm_g,
              lambda_q1, lambda_k1, lambda_q2, lambda_k2, subln_g, ab_w_out,
              ml_norm_g, ml_w_in, ml_conv_w, ml_conv_b, ml_gate_b, ml_out_norm_g, ml_w_out,
              ffn_norm_g, ffn_w_up, ffn_conv_w, ffn_conv_b, ffn_w_down):
    cos, sin = rope_tables(positions, DIFF_QKDIM)
    cos = cos[:, :, None, None, :]
    sin = sin[:, :, None, None, :]
    for layer in range(DEPTH):
        j = layer // 2
        if layer % 2 == 0:
            x = x + pool_diffattn_layer(
                x, cos, sin, layer, ab_norm_g[j], ab_w_in[j], pool_w[j], pool_scale[j],
                q_norm_g[j], k_norm_g[j], lambda_q1[j], lambda_k1[j], lambda_q2[j], lambda_k2[j],
                subln_g[j], ab_w_out[j])
        else:
            x = x + mlstm_layer(x, ml_norm_g[j], ml_w_in[j], ml_conv_w[j], ml_conv_b[j],
                                ml_gate_b[j], ml_out_norm_g[j], ml_w_out[j])
        x = x + conv_ffn(x, ffn_norm_g[layer], ffn_w_up[layer], ffn_conv_w[layer],
                         ffn_conv_b[layer], ffn_w_down[layer])
    return x
```

```python
import functools
import math

import jax
import jax.numpy as jnp
from jax import lax
from jax.experimental import pallas as pl
from jax.experimental.pallas import tpu as pltpu

F32 = jnp.float32
BF16 = jnp.bfloat16

D_MODEL = 1024
POOL_WINDOWS = (2, 4, 8, 16)
POOL_GROUP_DIM = 128
POOL_WIDTH = 512
DIFF_HEADS = 4
DIFF_WIDTH = 512
DIFF_VDIM = 128
DIFF_QKDIM = 64
ROPE_THETA = 10000.0
MLSTM_HEADS = 4
MLSTM_VDIM = 256
MLSTM_QKDIM = 128
MLSTM_QK_WIDTH = 512
MLSTM_V_WIDTH = 1024
MLSTM_CONV = 4
D_FF = 2816
FFN_CONV = 3
RMS_EPS = 1e-6
NEG_INF = -1e30

VMEM_LIMIT_BYTES = 56 * 1024 * 1024
LANES = 128

PROJ_TM = 1024
PROJ_TN = 512
OUT_TM = 512
FFN_TN = 256
ATT_TQ = 256
ATT_TK = 256
MLSTM_CHUNK = 256
GATE_PAD = LANES


def _params(*sem):
    return pltpu.CompilerParams(dimension_semantics=sem, vmem_limit_bytes=VMEM_LIMIT_BYTES)


def _shift_rows(x, k):
    rows = lax.broadcasted_iota(jnp.int32, x.shape, 0)
    return jnp.where(rows >= k, pltpu.roll(x, k, axis=0), 0.0)


def _rms_matmul_kernel(x_ref, g_ref, w_ref, o_ref, xn_ref):
    @pl.when(pl.program_id(1) == 0)
    def _():
        x = x_ref[...]
        ms = jnp.mean(x * x, axis=-1, keepdims=True)
        xn_ref[...] = (x * lax.rsqrt(ms + RMS_EPS) * g_ref[...]).astype(BF16)

    o_ref[...] = jnp.dot(xn_ref[...], w_ref[...], preferred_element_type=F32).astype(o_ref.dtype)


def rms_matmul(x, g, w, out_dtype):
    m, k = x.shape
    n = w.shape[1]
    return pl.pallas_call(
        _rms_matmul_kernel,
        out_shape=jax.ShapeDtypeStruct((m, n), out_dtype),
        grid=(m // PROJ_TM, n // PROJ_TN),
        in_specs=[
            pl.BlockSpec((PROJ_TM, k), lambda i, j: (i, 0)),
            pl.BlockSpec((1, k), lambda i, j: (0, 0)),
            pl.BlockSpec((k, PROJ_TN), lambda i, j: (0, j)),
        ],
        out_specs=pl.BlockSpec((PROJ_TM, PROJ_TN), lambda i, j: (i, j)),
        scratch_shapes=[pltpu.VMEM((PROJ_TM, k), BF16)],
        compiler_params=_params("parallel", "arbitrary"),
        name="rms_matmul",
    )(x, g, w)


def _matmul_kernel(a_ref, w_ref, o_ref):
    o_ref[...] = jnp.dot(a_ref[...], w_ref[...], preferred_element_type=F32).astype(o_ref.dtype)


def matmul(a, w, out_dtype, tn):
    m, k = a.shape
    n = w.shape[1]
    return pl.pallas_call(
        _matmul_kernel,
        out_shape=jax.ShapeDtypeStruct((m, n), out_dtype),
        grid=(m // PROJ_TM, n // tn),
        in_specs=[
            pl.BlockSpec((PROJ_TM, k), lambda i, j: (i, 0)),
            pl.BlockSpec((k, tn), lambda i, j: (0, j)),
        ],
        out_specs=pl.BlockSpec((PROJ_TM, tn), lambda i, j: (i, j)),
        compiler_params=_params("parallel", "arbitrary"),
        name="matmul",
    )(a, w)


def _matmul_res_norm_kernel(a_ref, w_ref, res_ref, g_ref, y_ref, yn_ref):
    y = jnp.dot(a_ref[...], w_ref[...], preferred_element_type=F32) + res_ref[...]
    y_ref[...] = y
    ms = jnp.mean(y * y, axis=-1, keepdims=True)
    yn_ref[...] = (y * lax.rsqrt(ms + RMS_EPS) * g_ref[...]).astype(BF16)


def _matmul_res_kernel(a_ref, w_ref, res_ref, y_ref):
    y_ref[...] = jnp.dot(a_ref[...], w_ref[...], preferred_element_type=F32) + res_ref[...]


def matmul_res_norm(a, w, res, g):
    m, k = a.shape
    n = w.shape[1]
    in_specs = [
        pl.BlockSpec((OUT_TM, k), lambda i: (i, 0)),
        pl.BlockSpec((k, n), lambda i: (0, 0)),
        pl.BlockSpec((OUT_TM, n), lambda i: (i, 0)),
    ]
    row_spec = pl.BlockSpec((OUT_TM, n), lambda i: (i, 0))
    if g is None:
        return pl.pallas_call(
            _matmul_res_kernel,
            out_shape=jax.ShapeDtypeStruct((m, n), F32),
            grid=(m // OUT_TM,),
            in_specs=in_specs,
            out_specs=row_spec,
            compiler_params=_params("parallel"),
            name="matmul_res",
        )(a, w, res)
    return pl.pallas_call(
        _matmul_res_norm_kernel,
        out_shape=(jax.ShapeDtypeStruct((m, n), F32), jax.ShapeDtypeStruct((m, n), BF16)),
        grid=(m // OUT_TM,),
        in_specs=in_specs + [pl.BlockSpec((1, n), lambda i: (0, 0))],
        out_specs=(row_spec, row_spec),
        compiler_params=_params("parallel"),
        name="matmul_res_norm",
    )(a, w, res, g)


def _pool_kernel(u_ref, w_ref, scale_ref, o_ref):
    s_len = u_ref.shape[0]
    t1 = (lax.broadcasted_iota(jnp.int32, (s_len, 1), 0) + 1).astype(F32)
    for g, win in enumerate(POOL_WINDOWS):
        cols = slice(g * POOL_GROUP_DIM, (g + 1) * POOL_GROUP_DIM)
        u = u_ref[:, cols]
        acc = u
        span = 1
        while span < win:
            acc = acc + _shift_rows(acc, span)
            span *= 2
        cnt = jnp.minimum(t1, float(win))
        r = acc / cnt - u
        y = jnp.dot(r.astype(BF16), w_ref[g], preferred_element_type=F32)
        o_ref[:, cols] = (y * scale_ref[:, cols]).astype(o_ref.dtype)


def pool_mixer(z, w_pool, scale):
    b, s, _ = z.shape
    return pl.pallas_call(
        _pool_kernel,
        out_shape=jax.ShapeDtypeStruct((b, s, POOL_WIDTH), BF16),
        grid=(b,),
        in_specs=[
            pl.BlockSpec((None, s, POOL_WIDTH), lambda i: (i, 0, 0)),
            pl.BlockSpec((len(POOL_WINDOWS), POOL_GROUP_DIM, POOL_GROUP_DIM), lambda i: (0, 0, 0)),
            pl.BlockSpec((1, POOL_WIDTH), lambda i: (0, 0)),
        ],
        out_specs=pl.BlockSpec((None, s, POOL_WIDTH), lambda i: (i, 0, 0)),
        compiler_params=_params("parallel"),
        name="pool_mixer",
    )(z, w_pool, scale)


def _group_mean_sq(x, gmat):
    xx = x * x
    hi = xx.astype(BF16)
    lo = (xx - hi.astype(F32)).astype(BF16)
    return (jnp.dot(hi, gmat, preferred_element_type=F32)
            + jnp.dot(lo, gmat, preferred_element_type=F32))


def _qk_prep_kernel(pos_ref, invf_ref, q_ref, k_ref, v_ref, qg_ref, kg_ref, gmat_ref,
                    qo_ref, ko_ref, vo_ref):
    ang = pos_ref[...].astype(F32) * invf_ref[...]
    lane = lax.broadcasted_iota(jnp.int32, ang.shape, 1)
    first_half = (lane % DIFF_QKDIM) < (DIFF_QKDIM // 2)
    cosv = jnp.cos(ang)
    sinv = jnp.where(first_half, -jnp.sin(ang), jnp.sin(ang))
    gmat = gmat_ref[...]

    def norm_rope(src_ref, gain_ref, dst_ref, scale):
        for h in range(DIFF_HEADS):
            cols = slice(h * LANES, (h + 1) * LANES)
            x = src_ref[:, cols]
            xn = x * lax.rsqrt(_group_mean_sq(x, gmat) + RMS_EPS) * gain_ref[...]
            rot = jnp.where(first_half,
                            pltpu.roll(xn, LANES - DIFF_QKDIM // 2, axis=1),
                            pltpu.roll(xn, DIFF_QKDIM // 2, axis=1))
            y = xn * cosv + rot * sinv
            if scale != 1.0:
                y = y * scale
            dst_ref[:, cols] = y.astype(dst_ref.dtype)

    norm_rope(q_ref, qg_ref, qo_ref, DIFF_QKDIM ** -0.5)
    norm_rope(k_ref, kg_ref, ko_ref, 1.0)
    vo_ref[...] = v_ref[...].astype(vo_ref.dtype)


def qk_prep(z, pos3, invf, qg, kg, gmat, ts=512):
    b, s, _ = z.shape
    blk = lambda c: pl.BlockSpec((None, ts, DIFF_WIDTH), lambda i, j, c=c: (i, j, c))
    vec = lambda: pl.BlockSpec((1, LANES), lambda i, j: (0, 0))
    out = jax.ShapeDtypeStruct((b, s, DIFF_WIDTH), BF16)
    return pl.pallas_call(
        _qk_prep_kernel,
        out_shape=(out, out, out),
        grid=(b, s // ts),
        in_specs=[
            pl.BlockSpec((None, ts, 1), lambda i, j: (i, j, 0)),
            vec(),
            blk(1), blk(2), blk(3),
            vec(), vec(),
            pl.BlockSpec((LANES, LANES), lambda i, j: (0, 0)),
        ],
        out_specs=(blk(0), blk(0), blk(0)),
        compiler_params=_params("parallel", "parallel"),
        name="qk_prep",
    )(pos3, invf, z, z, z, qg, kg, gmat)


def _diff_attn_kernel(lam_ref, q_ref, k_ref, v_ref, sg_ref, o_ref, *, lam_init):
    qi = pl.program_id(2)
    tq = q_ref.shape[0]
    d = DIFF_QKDIM
    q = q_ref[...]
    qs = (q[:, :d], q[:, d:])
    nt = (((1,), (1,)), ((), ()))

    def scores(c, k_blk):
        kc = k_blk[:, :d] if c == 0 else k_blk[:, d:]
        return lax.dot_general(qs[c], kc, nt, preferred_element_type=F32)

    def update(carry, s, v_blk):
        m, l, acc = carry
        m_new = jnp.maximum(m, jnp.max(s, axis=-1, keepdims=True))
        alpha = jnp.exp(m - m_new)
        p = jnp.exp(s - m_new)
        l_new = alpha * l + jnp.sum(p, axis=-1, keepdims=True)
        acc_new = alpha * acc + jnp.dot(p.astype(BF16), v_blk, preferred_element_type=F32)
        return m_new, l_new, acc_new

    def body(j, carry):
        off = pl.multiple_of(j * ATT_TK, ATT_TK)
        k_blk = k_ref[pl.ds(off, ATT_TK), :]
        v_blk = v_ref[pl.ds(off, ATT_TK), :]
        return tuple(update(carry[c], scores(c, k_blk), v_blk) for c in range(2))

    init_one = (jnp.full((tq, 1), NEG_INF, F32), jnp.zeros((tq, 1), F32),
                jnp.zeros((tq, DIFF_VDIM), F32))
    carry = lax.fori_loop(0, qi, body, (init_one, init_one))

    off = pl.multiple_of(qi * ATT_TK, ATT_TK)
    k_blk = k_ref[pl.ds(off, ATT_TK), :]
    v_blk = v_ref[pl.ds(off, ATT_TK), :]
    row = lax.broadcasted_iota(jnp.int32, (tq, ATT_TK), 0)
    col = lax.broadcasted_iota(jnp.int32, (tq, ATT_TK), 1)
    causal = col <= row
    fin = []
    for c in range(2):
        s = jnp.where(causal, scores(c, k_blk), NEG_INF)
        _, l, acc = update(carry[c], s, v_blk)
        fin.append(acc / l)

    o = fin[0] - lam_ref[0, 0] * fin[1]
    ms = jnp.mean(o * o, axis=-1, keepdims=True)
    o = o * lax.rsqrt(ms + RMS_EPS) * sg_ref[...] * (1.0 - lam_init)
    o_ref[...] = o.astype(o_ref.dtype)


def diff_attention(lam, qh, kh, vh, subln_g, lam_init):
    b, s, _ = qh.shape
    return pl.pallas_call(
        functools.partial(_diff_attn_kernel, lam_init=lam_init),
        out_shape=jax.ShapeDtypeStruct((b, s, DIFF_WIDTH), BF16),
        grid=(b, DIFF_HEADS, s // ATT_TQ),
        in_specs=[
            pl.BlockSpec(memory_space=pltpu.SMEM),
            pl.BlockSpec((None, ATT_TQ, LANES), lambda i, h, j: (i, j, h)),
            pl.BlockSpec((None, s, LANES), lambda i, h, j: (i, 0, h)),
            pl.BlockSpec((None, s, LANES), lambda i, h, j: (i, 0, h)),
            pl.BlockSpec((1, DIFF_VDIM), lambda i, h, j: (0, 0)),
        ],
        out_specs=pl.BlockSpec((None, ATT_TQ, LANES), lambda i, h, j: (i, j, h)),
        compiler_params=_params("parallel", "parallel", "arbitrary"),
        name="diff_attention",
    )(lam, qh, kh, vh, subln_g)


def _lambda_kernel(lq1_ref, lk1_ref, lq2_ref, lk2_ref, o_ref, *, lam_init):
    a = jnp.sum(lq1_ref[...] * lk1_ref[...], axis=-1, keepdims=True)
    b = jnp.sum(lq2_ref[...] * lk2_ref[...], axis=-1, keepdims=True)
    o_ref[...] = jnp.exp(a) - jnp.exp(b) + lam_init


def diff_lambda(lq1, lk1, lq2, lk2, lam_init):
    return pl.pallas_call(
        functools.partial(_lambda_kernel, lam_init=lam_init),
        out_shape=jax.ShapeDtypeStruct((1, 1), F32),
        name="diff_lambda",
    )(lq1, lk1, lq2, lk2)


def _ffn_up_kernel(xn_ref, wg_ref, wu_ref, cwg_ref, cwu_ref, cbg_ref, cbu_ref, o_ref):
    xn = xn_ref[...]

    def conv_half(w_ref, cw_ref, cb_ref):
        h = jnp.dot(xn, w_ref[...], preferred_element_type=F32)
        cw = cw_ref[...]
        return (h * cw[2:3, :] + _shift_rows(h, 1) * cw[1:2, :] + _shift_rows(h, 2) * cw[0:1, :]
                + cb_ref[...])

    gate = conv_half(wg_ref, cwg_ref, cbg_ref)
    up = conv_half(wu_ref, cwu_ref, cbu_ref)
    o_ref[...] = (gate * jax.nn.sigmoid(gate) * up).astype(o_ref.dtype)


def ffn_up(xn, w_up, conv_w, conv_b):
    b, s, k = xn.shape
    nt = D_FF // FFN_TN
    return pl.pallas_call(
        _ffn_up_kernel,
        out_shape=jax.ShapeDtypeStruct((b, s, D_FF), BF16),
        grid=(b, nt),
        in_specs=[
            pl.BlockSpec((None, s, k), lambda i, j: (i, 0, 0)),
            pl.BlockSpec((k, FFN_TN), lambda i, j: (0, j)),
            pl.BlockSpec((k, FFN_TN), lambda i, j: (0, j + nt)),
            pl.BlockSpec((FFN_CONV, FFN_TN), lambda i, j: (0, j)),
            pl.BlockSpec((FFN_CONV, FFN_TN), lambda i, j: (0, j + nt)),
            pl.BlockSpec((1, FFN_TN), lambda i, j: (0, j)),
            pl.BlockSpec((1, FFN_TN), lambda i, j: (0, j + nt)),
        ],
        out_specs=pl.BlockSpec((None, s, FFN_TN), lambda i, j: (i, 0, j)),
        compiler_params=_params("parallel", "arbitrary"),
        name="ffn_up",
    )(xn, w_up, w_up, conv_w, conv_w, conv_b, conv_b)


def _mlstm_kernel(q_ref, k_ref, v_ref, og_ref, gates_ref, gb_ref, cwq_ref, cwk_ref, cbq_ref,
                  cbk_ref, ng_ref, o_ref, qc_ref, kc_ref, li_ref, b_ref, c_ref, n_ref, m_ref):
    head = pl.program_id(1)
    s_len = q_ref.shape[0]
    L = MLSTM_CHUNK

    def conv_silu(x_ref, cw_ref, cb_ref):
        x = x_ref[...].astype(F32)
        cw = cw_ref[...]
        y = x * cw[3:4, :] + cb_ref[...]
        for j in range(1, MLSTM_CONV):
            y = y + _shift_rows(x, j) * cw[3 - j:4 - j, :]
        return y * jax.nn.sigmoid(y)

    qc_ref[...] = conv_silu(q_ref, cwq_ref, cbq_ref).astype(BF16)
    kc_ref[...] = (conv_silu(k_ref, cwk_ref, cbk_ref) * (MLSTM_QKDIM ** -0.5)).astype(BF16)

    gates = gates_ref[...] + gb_ref[...]
    lane = lax.broadcasted_iota(jnp.int32, gates.shape, 1)
    pos_in_chunk = lax.broadcasted_iota(jnp.int32, gates.shape, 0) % L
    bsum = jax.nn.log_sigmoid(gates)
    span = 1
    while span < L:
        bsum = bsum + jnp.where(pos_in_chunk >= span, pltpu.roll(bsum, span, axis=0), 0.0)
        span *= 2
    li_ref[...] = jnp.sum(jnp.where(lane == head, gates, 0.0), axis=-1, keepdims=True)
    b_ref[...] = jnp.sum(jnp.where(lane == head + MLSTM_HEADS, bsum, 0.0), axis=-1, keepdims=True)

    c_ref[...] = jnp.zeros_like(c_ref)
    n_ref[...] = jnp.zeros_like(n_ref)
    m_ref[...] = jnp.zeros_like(m_ref)

    r_i = lax.broadcasted_iota(jnp.int32, (L, L), 0)
    c_i = lax.broadcasted_iota(jnp.int32, (L, L), 1)
    tri = r_i >= c_i
    eye = r_i == c_i
    nt =(((1,), (1,)), ((), ()))
    tn = (((0,), (0,)), ((), ()))

    def chunk(ci, _):
        off = pl.multiple_of(ci * L, L)
        qc = qc_ref[pl.ds(off, L), :]
        kc = kc_ref[pl.ds(off, L), :]
        vc = v_ref[pl.ds(off, L), :]
        lic = li_ref[pl.ds(off, L), :]
        bcol = b_ref[pl.ds(off, L), :]
        b_last = b_ref[pl.ds(off + L - 1, 1), :]
        m_prev = m_ref[...]
        rcol = lic - bcol
        rrow = jnp.sum(jnp.where(eye, rcol, 0.0), axis=0, keepdims=True)
        dmat = jnp.where(tri, bcol + rrow, NEG_INF)
        a = bcol + m_prev
        m_t = jnp.maximum(a, jnp.max(dmat, axis=-1, keepdims=True))
        sc = lax.dot_general(qc, kc, nt, preferred_element_type=F32) * jnp.exp(dmat - m_t)
        inter = jnp.exp(a - m_t)
        num = (jnp.dot(sc.astype(BF16), vc, preferred_element_type=F32)
               + inter * jnp.dot(qc, c_ref[...].astype(BF16), preferred_element_type=F32))
        qn = jnp.sum(qc.astype(F32) * n_ref[...], axis=-1, keepdims=True)
        den = jnp.sum(sc, axis=-1, keepdims=True) + inter * qn
        h = num / jnp.maximum(jnp.abs(den), jnp.exp(-m_t))

        w = b_last + rcol
        m_new = jnp.maximum(b_last + m_prev, jnp.max(w, axis=0, keepdims=True))
        decay = jnp.exp(b_last + m_prev - m_new)
        kw = kc.astype(F32) * jnp.exp(w - m_new)
        c_ref[...] = decay * c_ref[...] + lax.dot_general(
            kw.astype(BF16), vc, tn, preferred_element_type=F32)
        n_ref[...] = decay * n_ref[...] + jnp.sum(kw, axis=0, keepdims=True)
        m_ref[...] = m_new

        ms = jnp.mean(h * h, axis=-1, keepdims=True)
        hn = h * lax.rsqrt(ms + RMS_EPS) * ng_ref[...]
        og = og_ref[pl.ds(off, L), :].astype(F32)
        o_ref[pl.ds(off, L), :] = (hn * jax.nn.sigmoid(og)).astype(o_ref.dtype)
        return 0

    lax.fori_loop(0, s_len // L, chunk, 0)


def mlstm(z, gates, gate_b, conv_w, conv_b, out_g):
    b, s, _ = z.shape
    qk_blocks = MLSTM_QK_WIDTH // MLSTM_QKDIM
    v_blocks0 = 2 * MLSTM_QK_WIDTH // MLSTM_VDIM
    o_blocks0 = v_blocks0 + MLSTM_V_WIDTH // MLSTM_VDIM
    return pl.pallas_call(
        _mlstm_kernel,
        out_shape=jax.ShapeDtypeStruct((b, s, MLSTM_V_WIDTH), BF16),
        grid=(b, MLSTM_HEADS),
        in_specs=[
            pl.BlockSpec((None, s, MLSTM_QKDIM), lambda i, h: (i, 0, h)),
            pl.BlockSpec((None, s, MLSTM_QKDIM), lambda i, h: (i, 0, h + qk_blocks)),
            pl.BlockSpec((None, s, MLSTM_VDIM), lambda i, h: (i, 0, h + v_blocks0)),
            pl.BlockSpec((None, s, MLSTM_VDIM), lambda i, h: (i, 0, h + o_blocks0)),
            pl.BlockSpec((None, s, GATE_PAD), lambda i, h: (i, 0, 0)),
            pl.BlockSpec((1, GATE_PAD), lambda i, h: (0, 0)),
            pl.BlockSpec((MLSTM_CONV, MLSTM_QKDIM), lambda i, h: (0, h)),
            pl.BlockSpec((MLSTM_CONV, MLSTM_QKDIM), lambda i, h: (0, h + qk_blocks)),
            pl.BlockSpec((1, MLSTM_QKDIM), lambda i, h: (0, h)),
            pl.BlockSpec((1, MLSTM_QKDIM), lambda i, h: (0, h + qk_blocks)),
            pl.BlockSpec((1, MLSTM_VDIM), lambda i, h: (0, h)),
        ],
        out_specs=pl.BlockSpec((None, s, MLSTM_VDIM), lambda i, h: (i, 0, h)),
        scratch_shapes=[
            pltpu.VMEM((s, MLSTM_QKDIM), BF16),
            pltpu.VMEM((s, MLSTM_QKDIM), BF16),
            pltpu.VMEM((s, 1), F32),
            pltpu.VMEM((s, 1), F32),
            pltpu.VMEM((MLSTM_QKDIM, MLSTM_VDIM), F32),
            pltpu.VMEM((1, MLSTM_QKDIM), F32),
            pltpu.VMEM((1, 1), F32),
        ],
        compiler_params=_params("parallel", "arbitrary"),
        name="mlstm",
    )(z, z, z, z, gates, gate_b, conv_w, conv_w, conv_b, conv_b, out_g)


def kernel(x, positions, ab_norm_g, ab_w_in, pool_w, pool_scale, q_norm_g, k_norm_g, lambda_q1, lambda_k1, lambda_q2, lambda_k2, subln_g, ab_w_out, ml_norm_g, ml_w_in, ml_conv_w, ml_conv_b, ml_gate_b, ml_out_norm_g, ml_w_out, ffn_norm_g, ffn_w_up, ffn_conv_w, ffn_conv_b, ffn_w_down):
    b, s, d = x.shape
    m = b * s
    row = lambda v: v.reshape(1, -1).astype(F32)
    x0 = x.reshape(m, d)

    lam_init = 0.8 - 0.6 * math.exp(-0.3 * 0)
    z = rms_matmul(x0, row(ab_norm_g[0]), ab_w_in[0].astype(BF16), F32).reshape(b, s, -1)
    a = pool_mixer(z, pool_w[0].astype(BF16), row(pool_scale[0]))

    inv_freq = ROPE_THETA ** (-jnp.arange(0, DIFF_QKDIM, 2, dtype=F32) / DIFF_QKDIM)
    invf = jnp.tile(inv_freq, LANES // (DIFF_QKDIM // 2)).reshape(1, LANES)
    grp = jnp.arange(LANES) // DIFF_QKDIM
    gmat = jnp.where(grp[:, None] == grp[None, :], 1.0 / DIFF_QKDIM, 0.0).astype(BF16)
    tile2 = lambda v: jnp.tile(v.astype(F32), LANES // DIFF_QKDIM).reshape(1, LANES)
    qh, kh, vh = qk_prep(z, positions.reshape(b, s, 1), invf, tile2(q_norm_g[0]),
                         tile2(k_norm_g[0]), gmat)
    lam = diff_lambda(row(lambda_q1[0]), row(lambda_k1[0]), row(lambda_q2[0]),
                      row(lambda_k2[0]), lam_init)
    att = diff_attention(lam, qh, kh, vh, row(subln_g[0]), lam_init)
    mix = jnp.concatenate([a, att], axis=-1).reshape(m, d)
    x1, xn1 = matmul_res_norm(mix, ab_w_out[0].astype(BF16), x0, row(ffn_norm_g[0]))

    gated = ffn_up(xn1.reshape(b, s, d), ffn_w_up[0].astype(BF16), ffn_conv_w[0], row(ffn_conv_b[0]))
    x2, xn2 = matmul_res_norm(gated.reshape(m, D_FF), ffn_w_down[0].astype(BF16), x1,
                              row(ml_norm_g[0]))

    n_main = 2 * MLSTM_QK_WIDTH + 2 * MLSTM_V_WIDTH
    w_in = ml_w_in[0]
    z2 = matmul(xn2, w_in[:, :n_main].astype(BF16), BF16, PROJ_TN).reshape(b, s, n_main)
    n_gates = 2 * MLSTM_HEADS
    w_gates = jnp.pad(w_in[:, n_main:], ((0, 0), (0, GATE_PAD - n_gates))).astype(BF16)
    gates = matmul(xn2, w_gates, F32, GATE_PAD).reshape(b, s, GATE_PAD)
    gate_b = jnp.pad(ml_gate_b[0].astype(F32), (0, GATE_PAD - n_gates)).reshape(1, GATE_PAD)
    hg = mlstm(z2, gates, gate_b, ml_conv_w[0], row(ml_conv_b[0]), row(ml_out_norm_g[0]))
    x3, xn3 = matmul_res_norm(hg.reshape(m, MLSTM_V_WIDTH), ml_w_out[0].astype(BF16), x2,
                              row(ffn_norm_g[1]))

    gated = ffn_up(xn3.reshape(b, s, d), ffn_w_up[1].astype(BF16), ffn_conv_w[1], row(ffn_conv_b[1]))
    x4 = matmul_res_norm(gated.reshape(m, D_FF), ffn_w_down[1].astype(BF16), x3, None)
    return x4.reshape(b, s, d)
```

```python
import functools
import math

import jax
import jax.numpy as jnp
from jax import lax
from jax.experimental import pallas as pl
from jax.experimental.pallas import tpu as pltpu

F32 = jnp.float32
BF16 = jnp.bfloat16

D_MODEL = 1024
POOL_WINDOWS = (2, 4, 8, 16)
POOL_GROUP_DIM = 128
POOL_WIDTH = 512
DIFF_HEADS = 4
DIFF_WIDTH = 512
DIFF_VDIM = 128
DIFF_QKDIM = 64
ROPE_THETA = 10000.0
MLSTM_HEADS = 4
MLSTM_VDIM = 256
MLSTM_QKDIM = 128
MLSTM_QK_WIDTH = 512
MLSTM_V_WIDTH = 1024
MLSTM_CONV = 4
D_FF = 2816
FFN_CONV = 3
RMS_EPS = 1e-6
NEG_INF = -1e30

VMEM_LIMIT_BYTES = 56 * 1024 * 1024
LANES = 128

PROJ_TM = 1024
PROJ_TN = 512
OUT_TM = 512
FFN_TN = 256
ATT_TQ = 256
ATT_TK = 256
MLSTM_CHUNK = 256
GATE_PAD = LANES


def _params(*sem):
    return pltpu.CompilerParams(dimension_semantics=sem, vmem_limit_bytes=VMEM_LIMIT_BYTES)


def _shift_rows(x, k):
    rows = lax.broadcasted_iota(jnp.int32, x.shape, 0)
    return jnp.where(rows >= k, pltpu.roll(x, k, axis=0), 0.0)


def _rms_matmul_kernel(x_ref, g_ref, w_ref, o_ref, xn_ref):
    @pl.when(pl.program_id(1) == 0)
    def _():
        x = x_ref[...]
        ms = jnp.mean(x * x, axis=-1, keepdims=True)
        xn_ref[...] = (x * lax.rsqrt(ms + RMS_EPS) * g_ref[...]).astype(BF16)

    o_ref[...] = jnp.dot(xn_ref[...], w_ref[...], preferred_element_type=F32).astype(o_ref.dtype)


def rms_matmul(x, g, w, out_dtype):
    m, k = x.shape
    n = w.shape[1]
    return pl.pallas_call(
        _rms_matmul_kernel,
        out_shape=jax.ShapeDtypeStruct((m, n), out_dtype),
        grid=(m // PROJ_TM, n // PROJ_TN),
        in_specs=[
            pl.BlockSpec((PROJ_TM, k), lambda i, j: (i, 0)),
            pl.BlockSpec((1, k), lambda i, j: (0, 0)),
            pl.BlockSpec((k, PROJ_TN), lambda i, j: (0, j)),
        ],
        out_specs=pl.BlockSpec((PROJ_TM, PROJ_TN), lambda i, j: (i, j)),
        scratch_shapes=[pltpu.VMEM((PROJ_TM, k), BF16)],
        compiler_params=_params("parallel", "arbitrary"),
        name="rms_matmul",
    )(x, g, w)


def _matmul_kernel(a_ref, w_ref, o_ref):
    o_ref[...] = jnp.dot(a_ref[...], w_ref[...], preferred_element_type=F32).astype(o_ref.dtype)


def matmul(a, w, out_dtype, tn):
    m, k = a.shape
    n = w.shape[1]
    return pl.pallas_call(
        _matmul_kernel,
        out_shape=jax.ShapeDtypeStruct((m, n), out_dtype),
        grid=(m // PROJ_TM, n // tn),
        in_specs=[
            pl.BlockSpec((PROJ_TM, k), lambda i, j: (i, 0)),
            pl.BlockSpec((k, tn), lambda i, j: (0, j)),
        ],
        out_specs=pl.BlockSpec((PROJ_TM, tn), lambda i, j: (i, j)),
        compiler_params=_params("parallel", "arbitrary"),
        name="matmul",
    )(a, w)


def _matmul_res_norm_kernel(a_ref, w_ref, res_ref, g_ref, y_ref, yn_ref):
    y = jnp.dot(a_ref[...], w_ref[...], preferred_element_type=F32) + res_ref[...]
    y_ref[...] = y
    ms = jnp.mean(y * y, axis=-1, keepdims=True)
    yn_ref[...] = (y * lax.rsqrt(ms + RMS_EPS) * g_ref[...]).astype(BF16)


def _matmul_res_kernel(a_ref, w_ref, res_ref, y_ref):
    y_ref[...] = jnp.dot(a_ref[...], w_ref[...], preferred_element_type=F32) + res_ref[...]


def matmul_res_norm(a, w, res, g):
    m, k = a.shape
    n = w.shape[1]
    in_specs = [
        pl.BlockSpec((OUT_TM, k), lambda i: (i, 0)),
        pl.BlockSpec((k, n), lambda i: (0, 0)),
        pl.BlockSpec((OUT_TM, n), lambda i: (i, 0)),
    ]
    row_spec = pl.BlockSpec((OUT_TM, n), lambda i: (i, 0))
    if g is None:
        return pl.pallas_call(
            _matmul_res_kernel,
            out_shape=jax.ShapeDtypeStruct((m, n), F32),
            grid=(m // OUT_TM,),
            in_specs=in_specs,
            out_specs=row_spec,
            compiler_params=_params("parallel"),
            name="matmul_res",
        )(a, w, res)
    return pl.pallas_call(
        _matmul_res_norm_kernel,
        out_shape=(jax.ShapeDtypeStruct((m, n), F32), jax.ShapeDtypeStruct((m, n), BF16)),
        grid=(m // OUT_TM,),
        in_specs=in_specs + [pl.BlockSpec((1, n), lambda i: (0, 0))],
        out_specs=(row_spec, row_spec),
        compiler_params=_params("parallel"),
        name="matmul_res_norm",
    )(a, w, res, g)


def _pool_kernel(u_ref, w_ref, scale_ref, o_ref):
    s_len = u_ref.shape[0]
    t1 = (lax.broadcasted_iota(jnp.int32, (s_len, 1), 0) + 1).astype(F32)
    for g, win in enumerate(POOL_WINDOWS):
        cols = slice(g * POOL_GROUP_DIM, (g + 1) * POOL_GROUP_DIM)
        u = u_ref[:, cols]
        acc = u
        span = 1
        while span < win:
            acc = acc + _shift_rows(acc, span)
            span *= 2
        cnt = jnp.minimum(t1, float(win))
        r = acc / cnt - u
        y = jnp.dot(r.astype(BF16), w_ref[g], preferred_element_type=F32)
        o_ref[:, cols] = (y * scale_ref[:, cols]).astype(o_ref.dtype)


def pool_mixer(z, w_pool, scale):
    b, s, _ = z.shape
    return pl.pallas_call(
        _pool_kernel,
        out_shape=jax.ShapeDtypeStruct((b, s, POOL_WIDTH), BF16),
        grid=(b,),
        in_specs=[
            pl.BlockSpec((None, s, POOL_WIDTH), lambda i: (i, 0, 0)),
            pl.BlockSpec((len(POOL_WINDOWS), POOL_GROUP_DIM, POOL_GROUP_DIM), lambda i: (0, 0, 0)),
            pl.BlockSpec((1, POOL_WIDTH), lambda i: (0, 0)),
        ],
        out_specs=pl.BlockSpec((None, s, POOL_WIDTH), lambda i: (i, 0, 0)),
        compiler_params=_params("parallel"),
        name="pool_mixer",
    )(z, w_pool, scale)


def _group_mean_sq(x, gmat):
    xx = x * x
    hi = xx.astype(BF16)
    lo = (xx - hi.astype(F32)).astype(BF16)
    return (jnp.dot(hi, gmat, preferred_element_type=F32)
            + jnp.dot(lo, gmat, preferred_element_type=F32))


def _qk_prep_kernel(pos_ref, invf_ref, q_ref, k_ref, v_ref, qg_ref, kg_ref, gmat_ref,
                    qo_ref, ko_ref, vo_ref):
    ang = pos_ref[...].astype(F32) * invf_ref[...]
    lane = lax.broadcasted_iota(jnp.int32, ang.shape, 1)
    first_half = (lane % DIFF_QKDIM) < (DIFF_QKDIM // 2)
    cosv = jnp.cos(ang)
    sinv = jnp.where(first_half, -jnp.sin(ang), jnp.sin(ang))
    gmat = gmat_ref[...]

    def norm_rope(src_ref, gain_ref, dst_ref, scale):
        for h in range(DIFF_HEADS):
            x = src_ref[:, h * LANES:(h + 1) * LANES]
            xn = x * lax.rsqrt(_group_mean_sq(x, gmat) + RMS_EPS) * gain_ref[...]
            rot = jnp.where(first_half,
                            pltpu.roll(xn, LANES - DIFF_QKDIM // 2, axis=1),
                            pltpu.roll(xn, DIFF_QKDIM // 2, axis=1))
            y = ((xn * cosv + rot * sinv) * scale).astype(dst_ref.dtype)
            dst_ref[2 * h] = y[:, :DIFF_QKDIM]
            dst_ref[2 * h + 1] = y[:, DIFF_QKDIM:]

    norm_rope(q_ref, qg_ref, qo_ref, DIFF_QKDIM ** -0.5 * math.log2(math.e))
    norm_rope(k_ref, kg_ref, ko_ref, 1.0)
    for h in range(DIFF_HEADS):
        vo_ref[h] = v_ref[:, h * DIFF_VDIM:(h + 1) * DIFF_VDIM].T.astype(vo_ref.dtype)


def qk_prep(z, pos3, invf, qg, kg, gmat, ts=512):
    b, s, _ = z.shape
    blk = lambda c: pl.BlockSpec((None, ts, DIFF_WIDTH), lambda i, j, c=c: (i, j, c))
    vec = lambda: pl.BlockSpec((1, LANES), lambda i, j: (0, 0))
    qk_out = jax.ShapeDtypeStruct((b, 2 * DIFF_HEADS, s, DIFF_QKDIM), BF16)
    qk_spec = pl.BlockSpec((None, 2 * DIFF_HEADS, ts, DIFF_QKDIM), lambda i, j: (i, 0, j, 0))
    return pl.pallas_call(
        _qk_prep_kernel,
        out_shape=(qk_out, qk_out, jax.ShapeDtypeStruct((b, DIFF_HEADS, DIFF_VDIM, s), BF16)),
        grid=(b, s // ts),
        in_specs=[
            pl.BlockSpec((None, ts, 1), lambda i, j: (i, j, 0)),
            vec(),
            blk(1), blk(2), blk(3),
            vec(), vec(),
            pl.BlockSpec((LANES, LANES), lambda i, j: (0, 0)),
        ],
        out_specs=(qk_spec, qk_spec,
                   pl.BlockSpec((None, DIFF_HEADS, DIFF_VDIM, ts), lambda i, j: (i, 0, 0, j))),
        compiler_params=_params("parallel", "parallel"),
        name="qk_prep",
    )(pos3, invf, z, z, z, qg, kg, gmat)


def _diff_attn_kernel(lam_ref, q_ref, k_ref, vt_ref, sg_ref, o_ref, s_ref, p_ref, vx_ref,
                      *, lam_init):
    s_len = q_ref.shape[1]
    tq, tk = ATT_TQ, ATT_TK
    nt = (((1,), (1,)), ((), ()))
    vx_ref[:DIFF_VDIM, :] = vt_ref[...]
    vx_ref[DIFF_VDIM:, :] = jnp.ones((vx_ref.shape[0] - DIFF_VDIM, s_len), vx_ref.dtype)
    key_i = lax.broadcasted_iota(jnp.int32, (tk, tq), 0)
    qry_i = lax.broadcasted_iota(jnp.int32, (tk, tq), 1)
    causal = key_i <= qry_i
    lam = lam_ref[0, 0]
    for qi in range(s_len // tq):
        kv = (qi + 1) * tk
        maps = []
        for c in range(2):
            qc = q_ref[c, qi * tq:(qi + 1) * tq, :]
            m = None
            for j in range(qi + 1):
                st = lax.dot_general(k_ref[c, j * tk:(j + 1) * tk, :], qc, nt,
                                     preferred_element_type=F32)
                if j == qi:
                    st = jnp.where(causal, st, NEG_INF)
                s_ref[c, j * tk:(j + 1) * tk, :] = st
                tile_max = jnp.max(st, axis=0, keepdims=True)
                m = tile_max if m is None else jnp.maximum(m, tile_max)
            for j in range(qi + 1):
                rows = slice(j * tk, (j + 1) * tk)
                p_ref[c, rows, :] = jnp.exp2(s_ref[c, rows, :] - m).astype(p_ref.dtype)
            acc = jnp.dot(vx_ref[:, :kv], p_ref[c, :kv, :], preferred_element_type=F32)
            maps.append(acc[:DIFF_VDIM] * (1.0 / acc[DIFF_VDIM:DIFF_VDIM + 1]))
        ot = maps[0] - lam * maps[1]
        ms = jnp.mean(ot * ot, axis=0, keepdims=True)
        ot = ot * lax.rsqrt(ms + RMS_EPS) * sg_ref[...] * (1.0 - lam_init)
        o_ref[qi * tq:(qi + 1) * tq, :] = ot.T.astype(o_ref.dtype)


def diff_attention(lam, qh, kh, vt, subln_g_col, lam_init):
    b, _, s, _ = qh.shape
    ones_rows = 16
    qk_spec = pl.BlockSpec((None, 2, s, DIFF_QKDIM), lambda i, h: (i, h, 0, 0))
    return pl.pallas_call(
        functools.partial(_diff_attn_kernel, lam_init=lam_init),
        out_shape=jax.ShapeDtypeStruct((b, s, DIFF_WIDTH), BF16),
        grid=(b, DIFF_HEADS),
        in_specs=[
            pl.BlockSpec(memory_space=pltpu.SMEM),
            qk_spec,
            qk_spec,
            pl.BlockSpec((None, None, DIFF_VDIM, s), lambda i, h: (i, h, 0, 0)),
            pl.BlockSpec((DIFF_VDIM, 1), lambda i, h: (0, 0)),
        ],
        out_specs=pl.BlockSpec((None, s, LANES), lambda i, h: (i, 0, h)),
        scratch_shapes=[
            pltpu.VMEM((2, s, ATT_TQ), F32),
            pltpu.VMEM((2, s, ATT_TQ), BF16),
            pltpu.VMEM((DIFF_VDIM + ones_rows, s), BF16),
        ],
        compiler_params=_params("parallel", "parallel"),
        name="diff_attention",
    )(lam, qh, kh, vt, subln_g_col)


def _lambda_kernel(lq1_ref, lk1_ref, lq2_ref, lk2_ref, o_ref, *, lam_init):
    a = jnp.sum(lq1_ref[...] * lk1_ref[...], axis=-1, keepdims=True)
    b = jnp.sum(lq2_ref[...] * lk2_ref[...], axis=-1, keepdims=True)
    o_ref[...] = jnp.exp(a) - jnp.exp(b) + lam_init


def diff_lambda(lq1, lk1, lq2, lk2, lam_init):
    return pl.pallas_call(
        functools.partial(_lambda_kernel, lam_init=lam_init),
        out_shape=jax.ShapeDtypeStruct((1, 1), F32),
        name="diff_lambda",
    )(lq1, lk1, lq2, lk2)


def _ffn_up_kernel(xn_ref, wg_ref, wu_ref, cwg_ref, cwu_ref, cbg_ref, cbu_ref, o_ref):
    xn = xn_ref[...]

    def conv_half(w_ref, cw_ref, cb_ref):
        h = jnp.dot(xn, w_ref[...], preferred_element_type=F32)
        cw = cw_ref[...]
        return (h * cw[2:3, :] + _shift_rows(h, 1) * cw[1:2, :] + _shift_rows(h, 2) * cw[0:1, :]
                + cb_ref[...])

    gate = conv_half(wg_ref, cwg_ref, cbg_ref)
    up = conv_half(wu_ref, cwu_ref, cbu_ref)
    o_ref[...] = (gate * jax.nn.sigmoid(gate) * up).astype(o_ref.dtype)


def ffn_up(xn, w_up, conv_w, conv_b):
    b, s, k = xn.shape
    nt = D_FF // FFN_TN
    return pl.pallas_call(
        _ffn_up_kernel,
        out_shape=jax.ShapeDtypeStruct((b, s, D_FF), BF16),
        grid=(b, nt),
        in_specs=[
            pl.BlockSpec((None, s, k), lambda i, j: (i, 0, 0)),
            pl.BlockSpec((k, FFN_TN), lambda i, j: (0, j)),
            pl.BlockSpec((k, FFN_TN), lambda i, j: (0, j + nt)),
            pl.BlockSpec((FFN_CONV, FFN_TN), lambda i, j: (0, j)),
            pl.BlockSpec((FFN_CONV, FFN_TN), lambda i, j: (0, j + nt)),
            pl.BlockSpec((1, FFN_TN), lambda i, j: (0, j)),
            pl.BlockSpec((1, FFN_TN), lambda i, j: (0, j + nt)),
        ],
        out_specs=pl.BlockSpec((None, s, FFN_TN), lambda i, j: (i, 0, j)),
        compiler_params=_params("parallel", "arbitrary"),
        name="ffn_up",
    )(xn, w_up, w_up, conv_w, conv_w, conv_b, conv_b)


def _mlstm_kernel(q_ref, k_ref, v_ref, og_ref, gates_ref, gb_ref, cwq_ref, cwk_ref, cbq_ref,
                  cbk_ref, ng_ref, o_ref, qc_ref, kc_ref, li_ref, b_ref, c_ref, n_ref, m_ref):
    head = pl.program_id(1)
    s_len = q_ref.shape[0]
    L = MLSTM_CHUNK

    def conv_silu(x_ref, cw_ref, cb_ref):
        x = x_ref[...].astype(F32)
        cw = cw_ref[...]
        y = x * cw[3:4, :] + cb_ref[...]
        for j in range(1, MLSTM_CONV):
            y = y + _shift_rows(x, j) * cw[3 - j:4 - j, :]
        return y * jax.nn.sigmoid(y)

    qc_ref[...] = conv_silu(q_ref, cwq_ref, cbq_ref).astype(BF16)
    kc_ref[...] = (conv_silu(k_ref, cwk_ref, cbk_ref) * (MLSTM_QKDIM ** -0.5)).astype(BF16)

    gates = gates_ref[...] + gb_ref[...]
    lane = lax.broadcasted_iota(jnp.int32, gates.shape, 1)
    pos_in_chunk = lax.broadcasted_iota(jnp.int32, gates.shape, 0) % L
    bsum = jax.nn.log_sigmoid(gates)
    span = 1
    while span < L:
        bsum = bsum + jnp.where(pos_in_chunk >= span, pltpu.roll(bsum, span, axis=0), 0.0)
        span *= 2
    li_ref[...] = jnp.sum(jnp.where(lane == head, gates, 0.0), axis=-1, keepdims=True)
    b_ref[...] = jnp.sum(jnp.where(lane == head + MLSTM_HEADS, bsum, 0.0), axis=-1, keepdims=True)

    c_ref[...] = jnp.zeros_like(c_ref)
    n_ref[...] = jnp.zeros_like(n_ref)
    m_ref[...] = jnp.zeros_like(m_ref)

    r_i = lax.broadcasted_iota(jnp.int32, (L, L), 0)
    c_i = lax.broadcasted_iota(jnp.int32, (L, L), 1)
    tri = r_i >= c_i
    eye = r_i == c_i
    nt =(((1,), (1,)), ((), ()))
    tn = (((0,), (0,)), ((), ()))

    def chunk(ci, _):
        off = pl.multiple_of(ci * L, L)
        qc = qc_ref[pl.ds(off, L), :]
        kc = kc_ref[pl.ds(off, L), :]
        vc = v_ref[pl.ds(off, L), :]
        lic = li_ref[pl.ds(off, L), :]
        bcol = b_ref[pl.ds(off, L), :]
        b_last = b_ref[pl.ds(off + L - 1, 1), :]
        m_prev = m_ref[...]
        rcol = lic - bcol
        rrow = jnp.sum(jnp.where(eye, rcol, 0.0), axis=0, keepdims=True)
        dmat = jnp.where(tri, bcol + rrow, NEG_INF)
        a = bcol + m_prev
        m_t = jnp.maximum(a, jnp.max(dmat, axis=-1, keepdims=True))
        sc = lax.dot_general(qc, kc, nt, preferred_element_type=F32) * jnp.exp(dmat - m_t)
        inter = jnp.exp(a - m_t)
        num = (jnp.dot(sc.astype(BF16), vc, preferred_element_type=F32)
               + inter * jnp.dot(qc, c_ref[...].astype(BF16), preferred_element_type=F32))
        qn = jnp.sum(qc.astype(F32) * n_ref[...], axis=-1, keepdims=True)
        den = jnp.sum(sc, axis=-1, keepdims=True) + inter * qn
        h = num / jnp.maximum(jnp.abs(den), jnp.exp(-m_t))

        w = b_last + rcol
        m_new = jnp.maximum(b_last + m_prev, jnp.max(w, axis=0, keepdims=True))
        decay = jnp.exp(b_last + m_prev - m_new)
        kw = kc.astype(F32) * jnp.exp(w - m_new)
        c_ref[...] = decay * c_ref[...] + lax.dot_general(
            kw.astype(BF16), vc, tn, preferred_element_type=F32)
        n_ref[...] = decay * n_ref[...] + jnp.sum(kw, axis=0, keepdims=True)
        m_ref[...] = m_new

        ms = jnp.mean(h * h, axis=-1, keepdims=True)
        hn = h * lax.rsqrt(ms + RMS_EPS) * ng_ref[...]
        og = og_ref[pl.ds(off, L), :].astype(F32)
        o_ref[pl.ds(off, L), :] = (hn * jax.nn.sigmoid(og)).astype(o_ref.dtype)
        return 0

    lax.fori_loop(0, s_len // L, chunk, 0)


def mlstm(z, gates, gate_b, conv_w, conv_b, out_g):
    b, s, _ = z.shape
    qk_blocks = MLSTM_QK_WIDTH // MLSTM_QKDIM
    v_blocks0 = 2 * MLSTM_QK_WIDTH // MLSTM_VDIM
    o_blocks0 = v_blocks0 + MLSTM_V_WIDTH // MLSTM_VDIM
    return pl.pallas_call(
        _mlstm_kernel,
        out_shape=jax.ShapeDtypeStruct((b, s, MLSTM_V_WIDTH), BF16),
        grid=(b, MLSTM_HEADS),
        in_specs=[
            pl.BlockSpec((None, s, MLSTM_QKDIM), lambda i, h: (i, 0, h)),
            pl.BlockSpec((None, s, MLSTM_QKDIM), lambda i, h: (i, 0, h + qk_blocks)),
            pl.BlockSpec((None, s, MLSTM_VDIM), lambda i, h: (i, 0, h + v_blocks0)),
            pl.BlockSpec((None, s, MLSTM_VDIM), lambda i, h: (i, 0, h + o_blocks0)),
            pl.BlockSpec((None, s, GATE_PAD), lambda i, h: (i, 0, 0)),
            pl.BlockSpec((1, GATE_PAD), lambda i, h: (0, 0)),
            pl.BlockSpec((MLSTM_CONV, MLSTM_QKDIM), lambda i, h: (0, h)),
            pl.BlockSpec((MLSTM_CONV, MLSTM_QKDIM), lambda i, h: (0, h + qk_blocks)),
            pl.BlockSpec((1, MLSTM_QKDIM), lambda i, h: (0, h)),
            pl.BlockSpec((1, MLSTM_QKDIM), lambda i, h: (0, h + qk_blocks)),
            pl.BlockSpec((1, MLSTM_VDIM), lambda i, h: (0, h)),
        ],
        out_specs=pl.BlockSpec((None, s, MLSTM_VDIM), lambda i, h: (i, 0, h)),
        scratch_shapes=[
            pltpu.VMEM((s, MLSTM_QKDIM), BF16),
            pltpu.VMEM((s, MLSTM_QKDIM), BF16),
            pltpu.VMEM((s, 1), F32),
            pltpu.VMEM((s, 1), F32),
            pltpu.VMEM((MLSTM_QKDIM, MLSTM_VDIM), F32),
            pltpu.VMEM((1, MLSTM_QKDIM), F32),
            pltpu.VMEM((1, 1), F32),
        ],
        compiler_params=_params("parallel", "arbitrary"),
        name="mlstm",
    )(z, z, z, z, gates, gate_b, conv_w, conv_w, conv_b, conv_b, out_g)


def kernel(x, positions, ab_norm_g, ab_w_in, pool_w, pool_scale, q_norm_g, k_norm_g, lambda_q1, lambda_k1, lambda_q2, lambda_k2, subln_g, ab_w_out, ml_norm_g, ml_w_in, ml_conv_w, ml_conv_b, ml_gate_b, ml_out_norm_g, ml_w_out, ffn_norm_g, ffn_w_up, ffn_conv_w, ffn_conv_b, ffn_w_down):
    b, s, d = x.shape
    m = b * s
    row = lambda v: v.reshape(1, -1).astype(F32)
    x0 = x.reshape(m, d)

    lam_init = 0.8 - 0.6 * math.exp(-0.3 * 0)
    z = rms_matmul(x0, row(ab_norm_g[0]), ab_w_in[0].astype(BF16), F32).reshape(b, s, -1)
    a = pool_mixer(z, pool_w[0].astype(BF16), row(pool_scale[0]))

    inv_freq = ROPE_THETA ** (-jnp.arange(0, DIFF_QKDIM, 2, dtype=F32) / DIFF_QKDIM)
    invf = jnp.tile(inv_freq, LANES // (DIFF_QKDIM // 2)).reshape(1, LANES)
    grp = jnp.arange(LANES) // DIFF_QKDIM
    gmat = jnp.where(grp[:, None] == grp[None, :], 1.0 / DIFF_QKDIM, 0.0).astype(BF16)
    tile2 = lambda v: jnp.tile(v.astype(F32), LANES // DIFF_QKDIM).reshape(1, LANES)
    qh, kh, vt = qk_prep(z, positions.reshape(b, s, 1), invf, tile2(q_norm_g[0]),
                         tile2(k_norm_g[0]), gmat)
    lam = diff_lambda(row(lambda_q1[0]), row(lambda_k1[0]), row(lambda_q2[0]),
                      row(lambda_k2[0]), lam_init)
    att = diff_attention(lam, qh, kh, vt, subln_g[0].astype(F32).reshape(DIFF_VDIM, 1), lam_init)
    mix = jnp.concatenate([a, att], axis=-1).reshape(m, d)
    x1, xn1 = matmul_res_norm(mix, ab_w_out[0].astype(BF16), x0, row(ffn_norm_g[0]))

    gated = ffn_up(xn1.reshape(b, s, d), ffn_w_up[0].astype(BF16), ffn_conv_w[0], row(ffn_conv_b[0]))
    x2, xn2 = matmul_res_norm(gated.reshape(m, D_FF), ffn_w_down[0].astype(BF16), x1,
                              row(ml_norm_g[0]))

    n_main = 2 * MLSTM_QK_WIDTH + 2 * MLSTM_V_WIDTH
    w_in = ml_w_in[0]
    z2 = matmul(xn2, w_in[:, :n_main].astype(BF16), BF16, PROJ_TN).reshape(b, s, n_main)
    n_gates = 2 * MLSTM_HEADS
    w_gates = jnp.pad(w_in[:, n_main:], ((0, 0), (0, GATE_PAD - n_gates))).astype(BF16)
    gates = matmul(xn2, w_gates, F32, GATE_PAD).reshape(b, s, GATE_PAD)
    gate_b = jnp.pad(ml_gate_b[0].astype(F32), (0, GATE_PAD - n_gates)).reshape(1, GATE_PAD)
    hg = mlstm(z2, gates, gate_b, ml_conv_w[0], row(ml_conv_b[0]), row(ml_out_norm_g[0]))
    x3, xn3 = matmul_res_norm(hg.reshape(m, MLSTM_V_WIDTH), ml_w_out[0].astype(BF16), x2,
                              row(ffn_norm_g[1]))

    gated = ffn_up(xn3.reshape(b, s, d), ffn_w_up[1].astype(BF16), ffn_conv_w[1], row(ffn_conv_b[1]))
    x4 = matmul_res_norm(gated.reshape(m, D_FF), ffn_w_down[1].astype(BF16), x3, None)
    return x4.reshape(b, s, d)
```

```python
import functools
import math

import jax
import jax.numpy as jnp
from jax import lax
from jax.experimental import pallas as pl
from jax.experimental.pallas import tpu as pltpu

F32 = jnp.float32
BF16 = jnp.bfloat16

D_MODEL = 1024
POOL_WINDOWS = (2, 4, 8, 16)
POOL_GROUP_DIM = 128
POOL_WIDTH = 512
DIFF_HEADS = 4
DIFF_WIDTH = 512
DIFF_VDIM = 128
DIFF_QKDIM = 64
ROPE_THETA = 10000.0
MLSTM_HEADS = 4
MLSTM_VDIM = 256
MLSTM_QKDIM = 128
MLSTM_QK_WIDTH = 512
MLSTM_V_WIDTH = 1024
MLSTM_CONV = 4
D_FF = 2816
FFN_CONV = 3
RMS_EPS = 1e-6
NEG_INF = -1e30

VMEM_LIMIT_BYTES = 56 * 1024 * 1024
LANES = 128

PROJ_TM = 1024
OUT_TM = 512
FFN_TN = 256
FFN_CHUNK = 512
FFN_HALO = 8
ATT_TQ = 256
ATT_TK = 256
MLSTM_CHUNK = 256
GATE_PAD = LANES


def _params(*sem, flags=None):
    return pltpu.CompilerParams(dimension_semantics=sem, vmem_limit_bytes=VMEM_LIMIT_BYTES,
                                flags=flags)


def _shift_rows(x, k):
    rows = lax.broadcasted_iota(jnp.int32, x.shape, 0)
    return jnp.where(rows >= k, pltpu.roll(x, k, axis=0), 0.0)


def _resident(shape):
    return pl.BlockSpec(shape, lambda i: (0,) * len(shape), pipeline_mode=pl.Buffered(1))


def _rms_matmul_kernel(x_ref, g_ref, w_ref, o_ref):
    x = x_ref[...]
    ms = jnp.mean(x * x, axis=-1, keepdims=True)
    xn = (x * lax.rsqrt(ms + RMS_EPS) * g_ref[...]).astype(BF16)
    o_ref[...] = jnp.dot(xn, w_ref[...], preferred_element_type=F32).astype(o_ref.dtype)


def rms_matmul(x, g, w, out_dtype):
    m, k = x.shape
    n = w.shape[1]
    return pl.pallas_call(
        _rms_matmul_kernel,
        out_shape=jax.ShapeDtypeStruct((m, n), out_dtype),
        grid=(m // PROJ_TM,),
        in_specs=[
            pl.BlockSpec((PROJ_TM, k), lambda i: (i, 0)),
            _resident((1, k)),
            _resident((k, n)),
        ],
        out_specs=pl.BlockSpec((PROJ_TM, n), lambda i: (i, 0)),
        compiler_params=_params("parallel"),
        name="rms_matmul",
    )(x, g, w)


def _matmul_split_kernel(a_ref, w_ref, wg_ref, o_ref, og_ref):
    a = a_ref[...]
    o_ref[...] = jnp.dot(a, w_ref[...], preferred_element_type=F32).astype(o_ref.dtype)
    og_ref[...] = jnp.dot(a, wg_ref[...], preferred_element_type=F32)


def matmul_split(a, w, w_gates):
    m, k = a.shape
    n, ng = w.shape[1], w_gates.shape[1]
    return pl.pallas_call(
        _matmul_split_kernel,
        out_shape=(jax.ShapeDtypeStruct((m, n), BF16), jax.ShapeDtypeStruct((m, ng), F32)),
        grid=(m // PROJ_TM,),
        in_specs=[
            pl.BlockSpec((PROJ_TM, k), lambda i: (i, 0)),
            _resident((k, n)),
            _resident((k, ng)),
        ],
        out_specs=(pl.BlockSpec((PROJ_TM, n), lambda i: (i, 0)),
                   pl.BlockSpec((PROJ_TM, ng), lambda i: (i, 0))),
        compiler_params=_params("parallel"),
        name="matmul_split",
    )(a, w, w_gates)


def _matmul_res_norm_kernel(a_ref, w_ref, res_ref, g_ref, y_ref, yn_ref):
    y = jnp.dot(a_ref[...], w_ref[...], preferred_element_type=F32) + res_ref[...]
    y_ref[...] = y
    ms = jnp.mean(y * y, axis=-1, keepdims=True)
    yn_ref[...] = (y * lax.rsqrt(ms + RMS_EPS) * g_ref[...]).astype(BF16)


def _matmul_res_kernel(a_ref, w_ref, res_ref, y_ref):
    y_ref[...] = jnp.dot(a_ref[...], w_ref[...], preferred_element_type=F32) + res_ref[...]


def matmul_res_norm(a, w, res, g):
    m, k = a.shape
    n = w.shape[1]
    in_specs = [
        pl.BlockSpec((OUT_TM, k), lambda i: (i, 0)),
        _resident((k, n)),
        pl.BlockSpec((OUT_TM, n), lambda i: (i, 0)),
    ]
    row_spec = pl.BlockSpec((OUT_TM, n), lambda i: (i, 0))
    if g is None:
        return pl.pallas_call(
            _matmul_res_kernel,
            out_shape=jax.ShapeDtypeStruct((m, n), F32),
            grid=(m // OUT_TM,),
            in_specs=in_specs,
            out_specs=row_spec,
            compiler_params=_params("parallel"),
            name="matmul_res",
        )(a, w, res)
    return pl.pallas_call(
        _matmul_res_norm_kernel,
        out_shape=(jax.ShapeDtypeStruct((m, n), F32), jax.ShapeDtypeStruct((m, n), BF16)),
        grid=(m // OUT_TM,),
        in_specs=in_specs + [_resident((1, n))],
        out_specs=(row_spec, row_spec),
        compiler_params=_params("parallel"),
        name="matmul_res_norm",
    )(a, w, res, g)


def _pool_kernel(u_ref, w_ref, scale_ref, o_ref):
    s_len = u_ref.shape[0]
    t1 = (lax.broadcasted_iota(jnp.int32, (s_len, 1), 0) + 1).astype(F32)
    for g, win in enumerate(POOL_WINDOWS):
        cols = slice(g * POOL_GROUP_DIM, (g + 1) * POOL_GROUP_DIM)
        u = u_ref[:, cols].astype(F32)
        acc = u
        span = 1
        while span < win:
            acc = acc + _shift_rows(acc, span)
            span *= 2
        cnt = jnp.minimum(t1, float(win))
        r = acc / cnt - u
        y = jnp.dot(r.astype(BF16), w_ref[g], preferred_element_type=F32)
        o_ref[:, cols] = (y * scale_ref[:, cols]).astype(o_ref.dtype)


def pool_mixer(z, w_pool, scale):
    b, s, _ = z.shape
    return pl.pallas_call(
        _pool_kernel,
        out_shape=jax.ShapeDtypeStruct((b, s, POOL_WIDTH), BF16),
        grid=(b,),
        in_specs=[
            pl.BlockSpec((None, s, POOL_WIDTH), lambda i: (i, 0, 0)),
            pl.BlockSpec((len(POOL_WINDOWS), POOL_GROUP_DIM, POOL_GROUP_DIM), lambda i: (0, 0, 0)),
            pl.BlockSpec((1, POOL_WIDTH), lambda i: (0, 0)),
        ],
        out_specs=pl.BlockSpec((None, s, POOL_WIDTH), lambda i: (i, 0, 0)),
        compiler_params=_params("parallel"),
        name="pool_mixer",
    )(z, w_pool, scale)


def _group_mean_sq(x, gmat):
    xx = x * x
    hi = xx.astype(BF16)
    lo = (xx - hi.astype(F32)).astype(BF16)
    return (jnp.dot(hi, gmat, preferred_element_type=F32)
            + jnp.dot(lo, gmat, preferred_element_type=F32))


def _qk_prep_kernel(pos_ref, invf_ref, q_ref, k_ref, v_ref, qg_ref, kg_ref, gmat_ref,
                    qo_ref, ko_ref, vo_ref):
    ang = pos_ref[...].astype(F32) * invf_ref[...]
    lane = lax.broadcasted_iota(jnp.int32, ang.shape, 1)
    first_half = (lane % DIFF_QKDIM) < (DIFF_QKDIM // 2)
    cosv = jnp.cos(ang)
    sinv = jnp.where(first_half, -jnp.sin(ang), jnp.sin(ang))
    gmat = gmat_ref[...]

    def norm_rope(src_ref, gain_ref, dst_ref, scale):
        for h in range(DIFF_HEADS):
            x = src_ref[:, h * LANES:(h + 1) * LANES].astype(F32)
            xn = x * lax.rsqrt(_group_mean_sq(x, gmat) + RMS_EPS) * gain_ref[...]
            rot = jnp.where(first_half,
                            pltpu.roll(xn, LANES - DIFF_QKDIM // 2, axis=1),
                            pltpu.roll(xn, DIFF_QKDIM // 2, axis=1))
            y = ((xn * cosv + rot * sinv) * scale).astype(dst_ref.dtype)
            dst_ref[2 * h] = y[:, :DIFF_QKDIM]
            dst_ref[2 * h + 1] = y[:, DIFF_QKDIM:]

    norm_rope(q_ref, qg_ref, qo_ref, DIFF_QKDIM ** -0.5 * math.log2(math.e))
    norm_rope(k_ref, kg_ref, ko_ref, 1.0)
    for h in range(DIFF_HEADS):
        vo_ref[h] = v_ref[:, h * DIFF_VDIM:(h + 1) * DIFF_VDIM].astype(F32).T.astype(vo_ref.dtype)


def qk_prep(z, pos3, invf, qg, kg, gmat, ts=512):
    b, s, _ = z.shape
    blk = lambda c: pl.BlockSpec((None, ts, DIFF_WIDTH), lambda i, j, c=c: (i, j, c))
    vec = lambda: pl.BlockSpec((1, LANES), lambda i, j: (0, 0))
    qk_out = jax.ShapeDtypeStruct((b, 2 * DIFF_HEADS, s, DIFF_QKDIM), BF16)
    qk_spec = pl.BlockSpec((None, 2 * DIFF_HEADS, ts, DIFF_QKDIM), lambda i, j: (i, 0, j, 0))
    return pl.pallas_call(
        _qk_prep_kernel,
        out_shape=(qk_out, qk_out, jax.ShapeDtypeStruct((b, DIFF_HEADS, DIFF_VDIM, s), BF16)),
        grid=(b, s // ts),
        in_specs=[
            pl.BlockSpec((None, ts, 1), lambda i, j: (i, j, 0)),
            vec(),
            blk(1), blk(2), blk(3),
            vec(), vec(),
            pl.BlockSpec((LANES, LANES), lambda i, j: (0, 0)),
        ],
        out_specs=(qk_spec, qk_spec,
                   pl.BlockSpec((None, DIFF_HEADS, DIFF_VDIM, ts), lambda i, j: (i, 0, 0, j))),
        compiler_params=_params("parallel", "parallel"),
        name="qk_prep",
    )(pos3, invf, z, z, z, qg, kg, gmat)


def _diff_attn_kernel(lam_ref, q_ref, k_ref, vt_ref, sg_ref, o_ref, s_ref, p_ref, vx_ref,
                      *, lam_init):
    s_len = q_ref.shape[1]
    tq, tk = ATT_TQ, ATT_TK
    nt = (((1,), (1,)), ((), ()))
    vx_ref[:DIFF_VDIM, :] = vt_ref[...]
    vx_ref[DIFF_VDIM:, :] = jnp.ones((vx_ref.shape[0] - DIFF_VDIM, s_len), vx_ref.dtype)
    key_i = lax.broadcasted_iota(jnp.int32, (tk, tq), 0)
    qry_i = lax.broadcasted_iota(jnp.int32, (tk, tq), 1)
    causal = key_i <= qry_i
    lam = lam_ref[0, 0]
    for qi in range(s_len // tq):
        kv = (qi + 1) * tk
        maps = []
        for c in range(2):
            qc = q_ref[c, qi * tq:(qi + 1) * tq, :]
            m = None
            for j in range(qi + 1):
                st = lax.dot_general(k_ref[c, j * tk:(j + 1) * tk, :], qc, nt,
                                     preferred_element_type=F32)
                if j == qi:
                    st = jnp.where(causal, st, NEG_INF)
                s_ref[c, j * tk:(j + 1) * tk, :] = st
                tile_max = jnp.max(st, axis=0, keepdims=True)
                m = tile_max if m is None else jnp.maximum(m, tile_max)
            for j in range(qi + 1):
                rows = slice(j * tk, (j + 1) * tk)
                p_ref[c, rows, :] = jnp.exp2(s_ref[c, rows, :] - m).astype(p_ref.dtype)
            acc = jnp.dot(vx_ref[:, :kv], p_ref[c, :kv, :], preferred_element_type=F32)
            maps.append(acc[:DIFF_VDIM] * (1.0 / acc[DIFF_VDIM:DIFF_VDIM + 1]))
        ot = maps[0] - lam * maps[1]
        ms = jnp.mean(ot * ot, axis=0, keepdims=True)
        ot = ot * lax.rsqrt(ms + RMS_EPS) * sg_ref[...] * (1.0 - lam_init)
        o_ref[qi * tq:(qi + 1) * tq, :] = ot.T.astype(o_ref.dtype)


def diff_attention(lam, qh, kh, vt, subln_g_col, lam_init):
    b, _, s, _ = qh.shape
    ones_rows = 16
    qk_spec = pl.BlockSpec((None, 2, s, DIFF_QKDIM), lambda i, h: (i, h, 0, 0))
    return pl.pallas_call(
        functools.partial(_diff_attn_kernel, lam_init=lam_init),
        out_shape=jax.ShapeDtypeStruct((b, s, DIFF_WIDTH), BF16),
        grid=(b, DIFF_HEADS),
        in_specs=[
            pl.BlockSpec(memory_space=pltpu.SMEM),
            qk_spec,
            qk_spec,
            pl.BlockSpec((None, None, DIFF_VDIM, s), lambda i, h: (i, h, 0, 0)),
            pl.BlockSpec((DIFF_VDIM, 1), lambda i, h: (0, 0)),
        ],
        out_specs=pl.BlockSpec((None, s, LANES), lambda i, h: (i, 0, h)),
        scratch_shapes=[
            pltpu.VMEM((2, s, ATT_TQ), F32),
            pltpu.VMEM((2, s, ATT_TQ), BF16),
            pltpu.VMEM((DIFF_VDIM + ones_rows, s), BF16),
        ],
        compiler_params=_params("parallel", "parallel"),
        name="diff_attention",
    )(lam, qh, kh, vt, subln_g_col)


def _lambda_kernel(lq1_ref, lk1_ref, lq2_ref, lk2_ref, o_ref, *, lam_init):
    a = jnp.sum(lq1_ref[...] * lk1_ref[...], axis=-1, keepdims=True)
    b = jnp.sum(lq2_ref[...] * lk2_ref[...], axis=-1, keepdims=True)
    o_ref[...] = jnp.exp(a) - jnp.exp(b) + lam_init


def diff_lambda(lq1, lk1, lq2, lk2, lam_init):
    return pl.pallas_call(
        functools.partial(_lambda_kernel, lam_init=lam_init),
        out_shape=jax.ShapeDtypeStruct((1, 1), F32),
        name="diff_lambda",
    )(lq1, lk1, lq2, lk2)


def _ffn_up_kernel(xn_ref, wg_ref, wu_ref, cwg_ref, cwu_ref, cbg_ref, cbu_ref, o_ref,
                   hg_ref, hu_ref):
    s_len = xn_ref.shape[0]
    zeros = jnp.zeros((FFN_HALO, hg_ref.shape[1]), F32)
    hg_ref[:FFN_HALO, :] = zeros
    hu_ref[:FFN_HALO, :] = zeros
    cwg, cwu = cwg_ref[...], cwu_ref[...]

    def conv(h_ref, cw, cb_ref, base):
        acc = h_ref[base:base + FFN_CHUNK, :] * cw[FFN_CONV - 1:FFN_CONV, :] + cb_ref[...]
        for j in range(1, FFN_CONV):
            acc = acc + (h_ref[base - j:base - j + FFN_CHUNK, :]
                         * cw[FFN_CONV - 1 - j:FFN_CONV - j, :])
        return acc

    def project(r):
        xr = xn_ref[r * FFN_CHUNK:(r + 1) * FFN_CHUNK, :]
        base = FFN_HALO + r * FFN_CHUNK
        hg_ref[base:base + FFN_CHUNK, :] = jnp.dot(xr, wg_ref[...], preferred_element_type=F32)
        hu_ref[base:base + FFN_CHUNK, :] = jnp.dot(xr, wu_ref[...], preferred_element_type=F32)

    n_chunks = s_len // FFN_CHUNK
    project(0)
    for r in range(n_chunks):
        if r + 1 < n_chunks:
            project(r + 1)
        base = FFN_HALO + r * FFN_CHUNK
        gate = conv(hg_ref, cwg, cbg_ref, base)
        up = conv(hu_ref, cwu, cbu_ref, base)
        o_ref[r * FFN_CHUNK:(r + 1) * FFN_CHUNK, :] = (
            gate * jax.nn.sigmoid(gate) * up).astype(o_ref.dtype)


def ffn_up(xn, w_up, conv_w, conv_b):
    b, s, k = xn.shape
    nt = D_FF // FFN_TN
    return pl.pallas_call(
        _ffn_up_kernel,
        out_shape=jax.ShapeDtypeStruct((b, s, D_FF), BF16),
        grid=(b, nt),
        in_specs=[
            pl.BlockSpec((None, s, k), lambda i, j: (i, 0, 0)),
            pl.BlockSpec((k, FFN_TN), lambda i, j: (0, j)),
            pl.BlockSpec((k, FFN_TN), lambda i, j: (0, j + nt)),
            pl.BlockSpec((FFN_CONV, FFN_TN), lambda i, j: (0, j)),
            pl.BlockSpec((FFN_CONV, FFN_TN), lambda i, j: (0, j + nt)),
            pl.BlockSpec((1, FFN_TN), lambda i, j: (0, j)),
            pl.BlockSpec((1, FFN_TN), lambda i, j: (0, j + nt)),
        ],
        out_specs=pl.BlockSpec((None, s, FFN_TN), lambda i, j: (i, 0, j)),
        scratch_shapes=[pltpu.VMEM((FFN_HALO + s, FFN_TN), F32),
                        pltpu.VMEM((FFN_HALO + s, FFN_TN), F32)],
        compiler_params=_params("parallel", "arbitrary"),
        name="ffn_up",
    )(xn, w_up, w_up, conv_w, conv_w, conv_b, conv_b)


def _mlstm_kernel(q_ref, k_ref, v_ref, og_ref, gates_ref, gb_ref, cwq_ref, cwk_ref, cbq_ref,
                  cbk_ref, ng_ref, o_ref, qc_ref, kc_ref, li_ref, b_ref, c_ref, n_ref, m_ref):
    head = pl.program_id(1)
    s_len = q_ref.shape[0]
    L = MLSTM_CHUNK

    def conv_silu(x_ref, cw_ref, cb_ref):
        x = x_ref[...].astype(F32)
        cw = cw_ref[...]
        y = x * cw[3:4, :] + cb_ref[...]
        for j in range(1, MLSTM_CONV):
            y = y + _shift_rows(x, j) * cw[3 - j:4 - j, :]
        return y * jax.nn.sigmoid(y)

    qc_ref[...] = conv_silu(q_ref, cwq_ref, cbq_ref).astype(BF16)
    kc_ref[...] = (conv_silu(k_ref, cwk_ref, cbk_ref) * (MLSTM_QKDIM ** -0.5)).astype(BF16)

    gates = gates_ref[...] + gb_ref[...]
    lane = lax.broadcasted_iota(jnp.int32, gates.shape, 1)
    pos_in_chunk = lax.broadcasted_iota(jnp.int32, gates.shape, 0) % L
    bsum = jax.nn.log_sigmoid(gates)
    span = 1
    while span < L:
        bsum = bsum + jnp.where(pos_in_chunk >= span, pltpu.roll(bsum, span, axis=0), 0.0)
        span *= 2
    li_ref[...] = jnp.sum(jnp.where(lane == head, gates, 0.0), axis=-1, keepdims=True)
    b_ref[...] = jnp.sum(jnp.where(lane == head + MLSTM_HEADS, bsum, 0.0), axis=-1, keepdims=True)

    c_ref[...] = jnp.zeros_like(c_ref)
    n_ref[...] = jnp.zeros_like(n_ref)
    m_ref[...] = jnp.zeros_like(m_ref)

    r_i = lax.broadcasted_iota(jnp.int32, (L, L), 0)
    c_i = lax.broadcasted_iota(jnp.int32, (L, L), 1)
    tri = r_i >= c_i
    eye = r_i == c_i
    nt =(((1,), (1,)), ((), ()))
    tn = (((0,), (0,)), ((), ()))

    def chunk(ci, _):
        off = pl.multiple_of(ci * L, L)
        qc = qc_ref[pl.ds(off, L), :]
        kc = kc_ref[pl.ds(off, L), :]
        vc = v_ref[pl.ds(off, L), :]
        lic = li_ref[pl.ds(off, L), :]
        bcol = b_ref[pl.ds(off, L), :]
        b_last = b_ref[pl.ds(off + L - 1, 1), :]
        m_prev = m_ref[...]
        rcol = lic - bcol
        rrow = jnp.sum(jnp.where(eye, rcol, 0.0), axis=0, keepdims=True)
        dmat = jnp.where(tri, bcol + rrow, NEG_INF)
        a = bcol + m_prev
        m_t = jnp.maximum(a, jnp.max(dmat, axis=-1, keepdims=True))
        sc = lax.dot_general(qc, kc, nt, preferred_element_type=F32) * jnp.exp(dmat - m_t)
        inter = jnp.exp(a - m_t)
        num = (jnp.dot(sc.astype(BF16), vc, preferred_element_type=F32)
               + inter * jnp.dot(qc, c_ref[...].astype(BF16), preferred_element_type=F32))
        qn = jnp.sum(qc.astype(F32) * n_ref[...], axis=-1, keepdims=True)
        den = jnp.sum(sc, axis=-1, keepdims=True) + inter * qn
        h = num / jnp.maximum(jnp.abs(den), jnp.exp(-m_t))

        w = b_last + rcol
        m_new = jnp.maximum(b_last + m_prev, jnp.max(w, axis=0, keepdims=True))
        decay = jnp.exp(b_last + m_prev - m_new)
        kw = kc.astype(F32) * jnp.exp(w - m_new)
        c_ref[...] = decay * c_ref[...] + lax.dot_general(
            kw.astype(BF16), vc, tn, preferred_element_type=F32)
        n_ref[...] = decay * n_ref[...] + jnp.sum(kw, axis=0, keepdims=True)
        m_ref[...] = m_new

        ms = jnp.mean(h * h, axis=-1, keepdims=True)
        hn = h * lax.rsqrt(ms + RMS_EPS) * ng_ref[...]
        og = og_ref[pl.ds(off, L), :].astype(F32)
        o_ref[pl.ds(off, L), :] = (hn * jax.nn.sigmoid(og)).astype(o_ref.dtype)
        return 0

    lax.fori_loop(0, s_len // L, chunk, 0)


def mlstm(z, gates, gate_b, conv_w, conv_b, out_g):
    b, s, _ = z.shape
    qk_blocks = MLSTM_QK_WIDTH // MLSTM_QKDIM
    v_blocks0 = 2 * MLSTM_QK_WIDTH // MLSTM_VDIM
    o_blocks0 = v_blocks0 + MLSTM_V_WIDTH // MLSTM_VDIM
    return pl.pallas_call(
        _mlstm_kernel,
        out_shape=jax.ShapeDtypeStruct((b, s, MLSTM_V_WIDTH), BF16),
        grid=(b, MLSTM_HEADS),
        in_specs=[
            pl.BlockSpec((None, s, MLSTM_QKDIM), lambda i, h: (i, 0, h)),
            pl.BlockSpec((None, s, MLSTM_QKDIM), lambda i, h: (i, 0, h + qk_blocks)),
            pl.BlockSpec((None, s, MLSTM_VDIM), lambda i, h: (i, 0, h + v_blocks0)),
            pl.BlockSpec((None, s, MLSTM_VDIM), lambda i, h: (i, 0, h + o_blocks0)),
            pl.BlockSpec((None, s, GATE_PAD), lambda i, h: (i, 0, 0)),
            pl.BlockSpec((1, GATE_PAD), lambda i, h: (0, 0)),
            pl.BlockSpec((MLSTM_CONV, MLSTM_QKDIM), lambda i, h: (0, h)),
            pl.BlockSpec((MLSTM_CONV, MLSTM_QKDIM), lambda i, h: (0, h + qk_blocks)),
            pl.BlockSpec((1, MLSTM_QKDIM), lambda i, h: (0, h)),
            pl.BlockSpec((1, MLSTM_QKDIM), lambda i, h: (0, h + qk_blocks)),
            pl.BlockSpec((1, MLSTM_VDIM), lambda i, h: (0, h)),
        ],
        out_specs=pl.BlockSpec((None, s, MLSTM_VDIM), lambda i, h: (i, 0, h)),
        scratch_shapes=[
            pltpu.VMEM((s, MLSTM_QKDIM), BF16),
            pltpu.VMEM((s, MLSTM_QKDIM), BF16),
            pltpu.VMEM((s, 1), F32),
            pltpu.VMEM((s, 1), F32),
            pltpu.VMEM((MLSTM_QKDIM, MLSTM_VDIM), F32),
            pltpu.VMEM((1, MLSTM_QKDIM), F32),
            pltpu.VMEM((1, 1), F32),
        ],
        compiler_params=_params("parallel", "arbitrary"),
        name="mlstm",
    )(z, z, z, z, gates, gate_b, conv_w, conv_w, conv_b, conv_b, out_g)


def kernel(x, positions, ab_norm_g, ab_w_in, pool_w, pool_scale, q_norm_g, k_norm_g, lambda_q1, lambda_k1, lambda_q2, lambda_k2, subln_g, ab_w_out, ml_norm_g, ml_w_in, ml_conv_w, ml_conv_b, ml_gate_b, ml_out_norm_g, ml_w_out, ffn_norm_g, ffn_w_up, ffn_conv_w, ffn_conv_b, ffn_w_down):
    b, s, d = x.shape
    m = b * s
    row = lambda v: v.reshape(1, -1).astype(F32)
    x0 = x.reshape(m, d)

    lam_init = 0.8 - 0.6 * math.exp(-0.3 * 0)
    z = rms_matmul(x0, row(ab_norm_g[0]), ab_w_in[0].astype(BF16), BF16).reshape(b, s, -1)
    a = pool_mixer(z, pool_w[0].astype(BF16), row(pool_scale[0]))

    inv_freq = ROPE_THETA ** (-jnp.arange(0, DIFF_QKDIM, 2, dtype=F32) / DIFF_QKDIM)
    invf = jnp.tile(inv_freq, LANES // (DIFF_QKDIM // 2)).reshape(1, LANES)
    grp = jnp.arange(LANES) // DIFF_QKDIM
    gmat = jnp.where(grp[:, None] == grp[None, :], 1.0 / DIFF_QKDIM, 0.0).astype(BF16)
    tile2 = lambda v: jnp.tile(v.astype(F32), LANES // DIFF_QKDIM).reshape(1, LANES)
    qh, kh, vt = qk_prep(z, positions.reshape(b, s, 1), invf, tile2(q_norm_g[0]),
                         tile2(k_norm_g[0]), gmat)
    lam = diff_lambda(row(lambda_q1[0]), row(lambda_k1[0]), row(lambda_q2[0]),
                      row(lambda_k2[0]), lam_init)
    att = diff_attention(lam, qh, kh, vt, subln_g[0].astype(F32).reshape(DIFF_VDIM, 1), lam_init)
    mix = jnp.concatenate([a, att], axis=-1).reshape(m, d)
    x1, xn1 = matmul_res_norm(mix, ab_w_out[0].astype(BF16), x0, row(ffn_norm_g[0]))

    gated = ffn_up(xn1.reshape(b, s, d), ffn_w_up[0].astype(BF16), ffn_conv_w[0], row(ffn_conv_b[0]))
    x2, xn2 = matmul_res_norm(gated.reshape(m, D_FF), ffn_w_down[0].astype(BF16), x1,
                              row(ml_norm_g[0]))

    n_main = 2 * MLSTM_QK_WIDTH + 2 * MLSTM_V_WIDTH
    w_in = ml_w_in[0]
    n_gates = 2 * MLSTM_HEADS
    w_gates = jnp.pad(w_in[:, n_main:], ((0, 0), (0, GATE_PAD - n_gates))).astype(BF16)
    z2, gates = matmul_split(xn2, w_in[:, :n_main].astype(BF16), w_gates)
    z2 = z2.reshape(b, s, n_main)
    gates = gates.reshape(b, s, GATE_PAD)
    gate_b = jnp.pad(ml_gate_b[0].astype(F32), (0, GATE_PAD - n_gates)).reshape(1, GATE_PAD)
    hg = mlstm(z2, gates, gate_b, ml_conv_w[0], row(ml_conv_b[0]), row(ml_out_norm_g[0]))
    x3, xn3 = matmul_res_norm(hg.reshape(m, MLSTM_V_WIDTH), ml_w_out[0].astype(BF16), x2,
                              row(ffn_norm_g[1]))

    gated = ffn_up(xn3.reshape(b, s, d), ffn_w_up[1].astype(BF16), ffn_conv_w[1], row(ffn_conv_b[1]))
    x4 = matmul_res_norm(gated.reshape(m, D_FF), ffn_w_down[1].astype(BF16), x3, None)
    return x4.reshape(b, s, d)
```
